```python
import math
import jax, jax.numpy as jnp
from jax import lax
import numpy as np

D_MODEL = 2048
BATCH = 2
SEQ = 16384
DEPTH = 2

GRID_W = 64
CTX_LEN = 256
HEAD_DIM = 128
N_HEAD_SLOTS = D_MODEL // HEAD_DIM
A_HEADS = N_HEAD_SLOTS // 2
A_KV_HEADS = A_HEADS // 4
B_HEADS = N_HEAD_SLOTS // 4
B_DH = HEAD_DIM // 2
C_HEADS = N_HEAD_SLOTS // 4
C_DK = HEAD_DIM
C_DV = HEAD_DIM
A_Q_W = A_HEADS * HEAD_DIM
A_KV_W = A_KV_HEADS * HEAD_DIM
B_W = B_HEADS * HEAD_DIM
C_W = C_HEADS * HEAD_DIM
IN_SPLIT_WIDTHS = (A_Q_W, A_KV_W, A_KV_W, B_W, B_W, B_W, C_W, C_W, C_W, C_W, C_W)
W_IN_COLS = A_Q_W + 2 * A_KV_W + 3 * B_W + 5 * C_W
Q_BLOCK = 128
HGRN_CHUNK = 64
ROPE_THETA = 10000.0
EPS = 1e-6
N_MOD = 6
PEER_HEADS = 8
PEER_NKEYS = 128
PEER_EXPERTS = PEER_NKEYS * PEER_NKEYS
PEER_TOPK = 16
PEER_DKEY = 128
PEER_TOKEN_BLOCK = 128

kernel_name = "hybrid_gqa_diffattn_hgrn2_peer_dit"


def rmsnorm(x, g):
    xf = x.astype(jnp.float32)
    y = xf * lax.rsqrt(jnp.mean(xf * xf, axis=-1, keepdims=True) + EPS)
    return (y * g.astype(jnp.float32)).astype(x.dtype)


def modulate(x, g, shift, scale):
    return rmsnorm(x, g) * (1 + scale) + shift


def axial_rope_tables(n_tokens, dim):
    rows = n_tokens // GRID_W
    row = jnp.repeat(jnp.arange(rows, dtype=jnp.float32), GRID_W)
    col = jnp.tile(jnp.arange(GRID_W, dtype=jnp.float32), rows)
    quarter = dim // 4
    freqs = ROPE_THETA ** (-jnp.arange(quarter, dtype=jnp.float32) / quarter)
    ang = jnp.stack([row[:, None] * freqs, col[:, None] * freqs], axis=1)
    ang = jnp.broadcast_to(ang[:, :, None, :], (n_tokens, 2, 2, quarter)).reshape(n_tokens, dim)
    return jnp.cos(ang), jnp.sin(ang)


def rotate_axial(x):
    lead = x.shape[:-1]
    d = x.shape[-1]
    xr = x.reshape(lead + (2, 2, d // 4))
    rot = jnp.stack([-xr[..., 1, :], xr[..., 0, :]], axis=-2)
    return rot.reshape(lead + (d,))


def headnorm_rope(x, g, cos=None, sin=None):
    xf = x.astype(jnp.float32)
    y = xf * lax.rsqrt(jnp.mean(xf * xf, axis=-1, keepdims=True) + EPS) * g.astype(jnp.float32)
    if cos is not None:
        y = y * cos + rotate_axial(y) * sin
    return y.astype(x.dtype)


def split_heads(t, n_heads):
    b, s, w = t.shape
    return t.reshape(b, s, n_heads, w // n_heads).transpose(0, 2, 1, 3)


def merge_heads(t):
    b, h, s, d = t.shape
    return t.transpose(0, 2, 1, 3).reshape(b, s, h * d)


def diff_heads(t, g, cos=None, sin=None):
    b, s, _ = t.shape
    y = t.reshape(b, s, B_HEADS, 2, B_DH).transpose(0, 2, 1, 3, 4)
    return headnorm_rope(y, g, cos, sin)


def gqa_block(qg, k, v):
    s = jnp.einsum('bkgqd,bkld->bkgql', qg, k).astype(jnp.float32) * (qg.shape[-1] ** -0.5)
    p = jax.nn.softmax(s, axis=-1).astype(v.dtype)
    return jnp.einsum('bkgql,bkld->bkgqd', p, v)


def gqa_attention(q, k, v, blocked):
    b, hq, t, dh = q.shape
    hkv = k.shape[1]
    qg = q.reshape(b, hkv, hq // hkv, t, dh)
    if not blocked:
        return gqa_block(qg, k, v).reshape(b, hq, t, dh)
    nb = t // Q_BLOCK
    qb = qg.reshape(b, hkv, hq // hkv, nb, Q_BLOCK, dh).transpose(3, 0, 1, 2, 4, 5)
    o = lax.map(lambda qi: gqa_block(qi, k, v), qb)
    return o.transpose(1, 2, 3, 0, 4, 5).reshape(b, hq, t, dh)


def diff_block(qi, k, v, lam):
    s = jnp.einsum('bhqmd,bhlmd->bhmql', qi, k).astype(jnp.float32) * (qi.shape[-1] ** -0.5)
    p = jax.nn.softmax(s, axis=-1)
    a = (p[:, :, 0] - lam * p[:, :, 1]).astype(v.dtype)
    return jnp.einsum('bhql,bhlv->bhqv', a, v)


def diff_attention(q, k, v, lam, blocked):
    if not blocked:
        return diff_block(q, k, v, lam)
    b, h, t, _, d = q.shape
    nb = t // Q_BLOCK
    qb = q.reshape(b, h, nb, Q_BLOCK, 2, d).transpose(2, 0, 1, 3, 4, 5)
    o = lax.map(lambda qi: diff_block(qi, k, v, lam), qb)
    return o.transpose(1, 2, 0, 3, 4).reshape(b, h, t, -1)


def hgrn_lower_bounds(raw):
    sm = jax.nn.softmax(raw.astype(jnp.float32), axis=0)
    return jnp.cumsum(sm, axis=0) - sm[0:1]


def hgrn_gates(z, lb):
    zf = z.astype(jnp.float32)
    f = lb + (1 - lb) * jax.nn.sigmoid(zf)
    k = (1 - lb) * jax.nn.sigmoid(-zf)
    return k, jnp.log(f)


def hgrn2_scan(q, k, log_f, v, s0):
    b, h, t, dk = q.shape
    dv = v.shape[-1]
    n = t // HGRN_CHUNK

    def chunks(a):
        return a.reshape(b, h, n, HGRN_CHUNK, a.shape[-1]).transpose(2, 0, 1, 3, 4)

    causal = jnp.tril(jnp.ones((HGRN_CHUNK, HGRN_CHUNK), dtype=bool))[:, :, None]

    def step(state, inp):
        qc, kc, lfc, vc = inp
        cum = jnp.cumsum(lfc, axis=-2)
        o_inter = jnp.einsum('bhtd,bhde->bhte', qc * jnp.exp(cum), state)
        rel = jnp.where(causal, cum[:, :, :, None, :] - cum[:, :, None, :, :], 0.0)
        decay = jnp.where(causal, jnp.exp(rel), 0.0)
        scores = jnp.einsum('bhtd,bhsd,bhtsd->bhts', qc, kc, decay)
        o_intra = jnp.einsum('bhts,bhse->bhte', scores, vc)
        last = cum[:, :, -1:, :]
        new_state = (jnp.exp(last[:, :, 0, :, None]) * state
                     + jnp.einsum('bhsd,bhse->bhde', kc * jnp.exp(last - cum), vc))
        return new_state, o_inter + o_intra

    s_fin, o = lax.scan(step, s0, (chunks(q), chunks(k), chunks(log_f), chunks(v)))
    return o.transpose(1, 2, 0, 3, 4).reshape(b, h, t, dv), s_fin


def hgrn_prep(parts, lb_f, lb_b):
    zq, zff, zfb, zi = parts
    q = jax.nn.silu(split_heads(zq, C_HEADS).astype(jnp.float32))
    kf, lff = hgrn_gates(split_heads(zff, C_HEADS), lb_f.reshape(C_HEADS, 1, C_DK))
    kb, lfb = hgrn_gates(split_heads(zfb, C_HEADS), lb_b.reshape(C_HEADS, 1, C_DK))
    v = split_heads(zi, C_HEADS).astype(jnp.float32)
    return q, kf, lff, kb, lfb, v


def hgrn_mixer(lat_parts, ctx_parts, lb_f, lb_b, norm_g):
    ql, kfl, lffl, kbl, lfbl, vl = hgrn_prep(lat_parts[:4], lb_f, lb_b)
    qc, kfc, lffc, kbc, lfbc, vc = hgrn_prep(ctx_parts[:4], lb_f, lb_b)
    s0 = jnp.zeros((qc.shape[0], C_HEADS, C_DK, C_DV), jnp.float32)
    rev = lambda a: jnp.flip(a, axis=2)
    o_fc, s_f = hgrn2_scan(qc, kfc, lffc, vc, s0)
    o_fl, _ = hgrn2_scan(ql, kfl, lffl, vl, s_f)
    o_bc, s_b = hgrn2_scan(rev(qc), rev(kbc), rev(lfbc), rev(vc), s0)
    o_bl, _ = hgrn2_scan(rev(ql), rev(kbl), rev(lfbl), rev(vl), s_b)

    def readout(o, zg):
        gate = jax.nn.silu(split_heads(zg, C_HEADS).astype(jnp.float32))
        return (rmsnorm(o, norm_g) * gate).astype(zg.dtype)

    return readout(o_fl + rev(o_bl), lat_parts[4]), readout(o_fc + rev(o_bc), ctx_parts[4])


def peer_ffn(h, wq, sub_k1, sub_k2, u, v):
    n_tok, d = h.shape
    half = PEER_DKEY // 2

    def block(hb):
        q = (hb @ wq).reshape(-1, PEER_HEADS, 2, half)
        s1 = jnp.einsum('thd,hkd->thk', q[:, :, 0], sub_k1).astype(jnp.float32)
        s2 = jnp.einsum('thd,hkd->thk', q[:, :, 1], sub_k2).astype(jnp.float32)
        top1, idx1 = lax.top_k(s1, PEER_TOPK)
        top2, idx2 = lax.top_k(s2, PEER_TOPK)
        cand = (top1[..., :, None] + top2[..., None, :]).reshape(top1.shape[:-1] + (PEER_TOPK * PEER_TOPK,))
        best, ci = lax.top_k(cand, PEER_TOPK)
        e1 = jnp.take_along_axis(idx1, ci // PEER_TOPK, axis=-1)
        e2 = jnp.take_along_axis(idx2, ci % PEER_TOPK, axis=-1)
        expert = e1 * PEER_NKEYS + e2
        gate = jax.nn.softmax(best, axis=-1)
        pre = jnp.einsum('thkd,td->thk', u[expert], hb).astype(jnp.float32)
        w = (gate * jax.nn.gelu(pre, approximate=False)).astype(hb.dtype)
        return jnp.einsum('thk,thkd->td', w, v[expert])

    out = lax.map(block, h.reshape(n_tok // PEER_TOKEN_BLOCK, PEER_TOKEN_BLOCK, d))
    return out.reshape(n_tok, d)


def hybrid_layer(x, ctx, c_act, cctx_act, p, lam_init, rope, need_ctx):
    b, t, d = x.shape
    cos_a, sin_a, cos_b, sin_b = rope
    mod = c_act @ p['w_ada'] + p['b_ada']
    mod_c = cctx_act @ p['w_ada'] + p['b_ada']
    sh1, sc1, g1, sh2, sc2, g2 = jnp.split(mod[:, None, :], N_MOD, axis=-1)
    sh1c, sc1c, g1c, sh2c, sc2c, g2c = jnp.split(mod_c, N_MOD, axis=-1)

    offs = np.cumsum(IN_SPLIT_WIDTHS)[:-1].tolist()
    h = modulate(x, p['norm1_g'], sh1, sc1)
    hc = modulate(ctx, p['norm1_g'], sh1c, sc1c)
    pl = jnp.split(h @ p['w_in'], offs, axis=-1)
    pc = jnp.split(hc @ p['w_in'], offs, axis=-1)

    qa = headnorm_rope(split_heads(pl[0], A_HEADS), p['qn_a'], cos_a, sin_a)
    ka = headnorm_rope(split_heads(pl[1], A_KV_HEADS), p['kn_a'], cos_a, sin_a)
    va = split_heads(pl[2], A_KV_HEADS)
    ka_c = headnorm_rope(split_heads(pc[1], A_KV_HEADS), p['kn_a'])
    va_c = split_heads(pc[2], A_KV_HEADS)
    oa = gqa_attention(qa, jnp.concatenate([ka_c, ka], axis=2), jnp.concatenate([va_c, va], axis=2), True)

    lam = (jnp.exp(jnp.sum(p['lam_q1'].astype(jnp.float32) * p['lam_k1'].astype(jnp.float32)))
           - jnp.exp(jnp.sum(p['lam_q2'].astype(jnp.float32) * p['lam_k2'].astype(jnp.float32))) + lam_init)
    qb = diff_heads(pl[3], p['qn_b'], cos_b[:, None, :], sin_b[:, None, :])
    kb = diff_heads(pl[4], p['kn_b'], cos_b[:, None, :], sin_b[:, None, :])
    vb = split_heads(pl[5], B_HEADS)
    kb_c = diff_heads(pc[4], p['kn_b'])
    vb_c = split_heads(pc[5], B_HEADS)
    ob = diff_attention(qb, jnp.concatenate([kb_c, kb], axis=2), jnp.concatenate([vb_c, vb], axis=2), lam, True)
    ob = rmsnorm(ob, p['subln_g']) * (1 - lam_init)

    oc, oc_c = hgrn_mixer(pl[6:11], pc[6:11], p['lb_f'], p['lb_b'], p['hgrn_g'])

    mix = jnp.concatenate([merge_heads(oa), merge_heads(ob), merge_heads(oc)], axis=-1) @ p['w_out']
    x = x + g1 * mix
    h2 = modulate(x, p['norm2_g'], sh2, sc2)
    x = x + g2 * peer_ffn(h2.reshape(b * t, d), p['peer_wq'], p['peer_k1'], p['peer_k2'],
                          p['peer_u'], p['peer_v']).reshape(b, t, d)

    if need_ctx:
        qa_c = headnorm_rope(split_heads(pc[0], A_HEADS), p['qn_a'])
        oa_c = gqa_attention(qa_c, ka_c, va_c, False)
        qb_c = diff_heads(pc[3], p['qn_b'])
        ob_c = rmsnorm(diff_attention(qb_c, kb_c, vb_c, lam, False), p['subln_g']) * (1 - lam_init)
        mix_c = jnp.concatenate([merge_heads(oa_c), merge_heads(ob_c), merge_heads(oc_c)], axis=-1) @ p['w_out']
        ctx = ctx + g1c * mix_c
        h2c = modulate(ctx, p['norm2_g'], sh2c, sc2c)
        ctx = ctx + g2c * peer_ffn(h2c.reshape(-1, d), p['peer_wq'], p['peer_k1'], p['peer_k2'],
                                   p['peer_u'], p['peer_v']).reshape(ctx.shape)
    return x, ctx


def setup_inputs(seed: int = 0) -> dict:
    key = jax.random.key(seed)
    ks = jax.random.split(key, 27)
    f32 = jnp.float32

    def nrm(k, shape, scale):
        return jax.random.normal(k, shape, f32) * scale

    def gain(k, shape):
        return 1.0 + 0.02 * jax.random.normal(k, shape, f32)

    return {
        'x': nrm(ks[0], (BATCH, SEQ, D_MODEL), 1.0),
        'c': nrm(ks[1], (BATCH, D_MODEL), 1.0),
        'ctx': nrm(ks[2], (BATCH, CTX_LEN, D_MODEL), 1.0),
        'c_ctx': nrm(ks[3], (D_MODEL,), 1.0),
        'w_ada': nrm(ks[4], (DEPTH, D_MODEL, N_MOD * D_MODEL), 0.5 * D_MODEL ** -0.5),
        'b_ada': nrm(ks[5], (DEPTH, N_MOD * D_MODEL), 0.01),
        'norm1_g': gain(ks[6], (DEPTH, D_MODEL)),
        'w_in': nrm(ks[7], (DEPTH, D_MODEL, W_IN_COLS), D_MODEL ** -0.5),
        'qn_a': gain(ks[8], (DEPTH, HEAD_DIM)),
        'kn_a': gain(ks[9], (DEPTH, HEAD_DIM)),
        'qn_b': gain(ks[10], (DEPTH, B_DH)),
        'kn_b': gain(ks[11], (DEPTH, B_DH)),
        'lam_q1': nrm(ks[12], (DEPTH, B_DH), 0.1),
        'lam_k1': nrm(ks[13], (DEPTH, B_DH), 0.1),
        'lam_q2': nrm(ks[14], (DEPTH, B_DH), 0.1),
        'lam_k2': nrm(ks[15], (DEPTH, B_DH), 0.1),
        'subln_g': gain(ks[16], (DEPTH, HEAD_DIM)),
        'hgrn_lb_f': nrm(ks[17], (DEPTH, C_W), 0.1),
        'hgrn_lb_b': nrm(ks[18], (DEPTH, C_W), 0.1),
        'hgrn_g': gain(ks[19], (DEPTH, C_DV)),
        'w_out': nrm(ks[20], (DEPTH, D_MODEL, D_MODEL), D_MODEL ** -0.5),
        'norm2_g': gain(ks[21], (DEPTH, D_MODEL)),
        'peer_wq': nrm(ks[22], (DEPTH, D_MODEL, PEER_HEADS * PEER_DKEY), D_MODEL ** -0.5),
        'peer_k1': nrm(ks[23], (DEPTH, PEER_HEADS, PEER_NKEYS, PEER_DKEY // 2), (PEER_DKEY // 2) ** -0.5),
        'peer_k2': nrm(ks[24], (DEPTH, PEER_HEADS, PEER_NKEYS, PEER_DKEY // 2), (PEER_DKEY // 2) ** -0.5),
        'peer_u': nrm(ks[25], (DEPTH, PEER_EXPERTS, D_MODEL), D_MODEL ** -0.5),
        'peer_v': nrm(ks[26], (DEPTH, PEER_EXPERTS, D_MODEL), PEER_HEADS ** -0.5),
    }


def reference(x, c, ctx, c_ctx, w_ada, b_ada, norm1_g, w_in, qn_a, kn_a, qn_b, kn_b,
              lam_q1, lam_k1, lam_q2, lam_k2, subln_g, hgrn_lb_f, hgrn_lb_b, hgrn_g,
              w_out, norm2_g, peer_wq, peer_k1, peer_k2, peer_u, peer_v):
    n_lat = x.shape[1]
    cos_a, sin_a = axial_rope_tables(n_lat, HEAD_DIM)
    cos_b, sin_b = axial_rope_tables(n_lat, B_DH)
    rope = (cos_a, sin_a, cos_b, sin_b)
    c_act = jax.nn.silu(c)
    cctx_act = jax.nn.silu(c_ctx)
    lb_f_all = hgrn_lower_bounds(hgrn_lb_f)
    lb_b_all = hgrn_lower_bounds(hgrn_lb_b)
    for l in range(DEPTH):
        p = dict(w_ada=w_ada[l], b_ada=b_ada[l], norm1_g=norm1_g[l], w_in=w_in[l],
                 qn_a=qn_a[l], kn_a=kn_a[l], qn_b=qn_b[l], kn_b=kn_b[l],
                 lam_q1=lam_q1[l], lam_k1=lam_k1[l], lam_q2=lam_q2[l], lam_k2=lam_k2[l],
                 subln_g=subln_g[l], lb_f=lb_f_all[l], lb_b=lb_b_all[l], hgrn_g=hgrn_g[l],
                 w_out=w_out[l], norm2_g=norm2_g[l], peer_wq=peer_wq[l], peer_k1=peer_k1[l],
                 peer_k2=peer_k2[l], peer_u=peer_u[l], peer_v=peer_v[l])
        lam_init = 0.8 - 0.6 * math.exp(-0.3 * l)
        x, ctx = hybrid_layer(x, ctx, c_act, cctx_act, p, lam_init, rope, l < DEPTH - 1)
    return x
```

```python
import functools
import math

import numpy as np
import jax
import jax.numpy as jnp
from jax import lax
from jax.experimental import pallas as pl
from jax.experimental.pallas import tpu as pltpu

F32 = jnp.float32
BF16 = jnp.bfloat16

D_MODEL = 2048
HEAD_DIM = 128
GRID_W = 64
A_HEADS = 8
A_KV_HEADS = 2
A_GROUP = A_HEADS // A_KV_HEADS
B_HEADS = 4
B_DH = 64
C_HEADS = 4
C_W = C_HEADS * HEAD_DIM
ROPE_THETA = 10000.0
EPS = 1e-6
N_MOD = 6
PEER_HEADS = 8
PEER_NKEYS = 128
PEER_TOPK = 16
PEER_HALF = 64
PEER_EXPERTS = PEER_NKEYS * PEER_NKEYS
PEER_QW = PEER_HEADS * 2 * PEER_HALF

COL_TILE = 512
N_COL_TILES = 11
HGRN_CHUNK = 128
V7X_VMEM_LIMIT = 56 * 1024 * 1024


def _pick(n, candidates):
    for c in candidates:
        if n % c == 0:
            return c
    raise ValueError(f"no block size in {candidates} divides {n}")


def _params(sem):
    return pltpu.CompilerParams(dimension_semantics=sem, vmem_limit_bytes=V7X_VMEM_LIMIT)


def _dot(a, b):
    return jnp.dot(a, b, preferred_element_type=F32)


def _dot_nt(a, b):
    return lax.dot_general(a, b, (((1,), (1,)), ((), ())), preferred_element_type=F32)


def _split3(x):
    hi = x.astype(BF16)
    r1 = x - hi.astype(F32)
    mid = r1.astype(BF16)
    lo = (r1 - mid.astype(F32)).astype(BF16)
    return hi, mid, lo


def _dot_01(p, x):
    hi, mid, lo = _split3(x)
    return _dot(p, hi) + _dot(p, mid) + _dot(p, lo)


def _sigmoid(z):
    return 1.0 / (1.0 + jnp.exp(-z))


def _mod_rows(mod_ref, row0, n_rows, n_ctx):
    rows = row0 + lax.broadcasted_iota(jnp.int32, (n_rows, 1), 0)
    is_ctx = rows < n_ctx
    return is_ctx, mod_ref[0:1, :], mod_ref[1:2, :]


def _sel(is_ctx, m_ctx, m_lat, k):
    d = D_MODEL
    return jnp.where(is_ctx, m_ctx[:, k * d:(k + 1) * d], m_lat[:, k * d:(k + 1) * d])


def _rms(x, width):
    return lax.rsqrt(jnp.sum(x * x, axis=-1, keepdims=True) * (1.0 / width) + EPS)


def _ada_kernel(c_ref, w_ref, b_ref, o_ref):
    c = c_ref[...]
    act = c * _sigmoid(c)
    o_ref[...] = _dot_hi(act, w_ref[...]) + b_ref[...]


def _dot_hi(a, w):
    a0, a1, a2 = _split3(a)
    w0, w1, w2 = _split3(w)
    return (_dot(a0, w0) + (_dot(a0, w1) + _dot(a1, w0))
            + (_dot(a1, w1) + _dot(a0, w2) + _dot(a2, w0)))


def _ada_call(cvec, w_ada, b_ada):
    depth, d, n = w_ada.shape
    tn = _pick(n, (1024, 512, 128))
    return pl.pallas_call(
        _ada_kernel,
        out_shape=jax.ShapeDtypeStruct((depth, 8, n), F32),
        grid=(depth, n // tn),
        in_specs=[pl.BlockSpec((8, d), lambda l, j: (0, 0)),
                  pl.BlockSpec((None, d, tn), lambda l, j: (l, 0, j)),
                  pl.BlockSpec((None, 1, tn), lambda l, j: (l, 0, j))],
        out_specs=pl.BlockSpec((None, 8, tn), lambda l, j: (l, 0, j)),
        compiler_params=_params(("arbitrary", "arbitrary")),
        name="adaln_mod",
    )(cvec, w_ada, b_ada.reshape(depth, 1, n))


def _rope(y, cos, s_up, s_dn, half):
    return y * cos + pltpu.roll(y, 128 - half, 1) * s_up + pltpu.roll(y, half, 1) * s_dn


def _in_kernel(x_ref, mod_ref, g_ref, w_ref, ca_ref, sau_ref, sad_ref, cb_ref, sbu_ref, sbd_ref,
               qna_ref, kna_ref, qnb_ref, knb_ref, lbf_ref, lbb_ref,
               qa_ref, kva_ref, qb_ref, kb_ref, vb_ref, cq_ref, ck_ref, cl_ref, cv_ref, cg_ref,
               h_scr, acc_scr, *, tm, n_ctx):
    i = pl.program_id(1)
    j = pl.program_id(2)

    @pl.when(j == 0)
    def _():
        x = x_ref[...]
        is_ctx, m_ctx, m_lat = _mod_rows(mod_ref, i * tm, tm, n_ctx)
        y = x * _rms(x, D_MODEL) * g_ref[...]
        h = y * (1.0 + _sel(is_ctx, m_ctx, m_lat, 1)) + _sel(is_ctx, m_ctx, m_lat, 0)
        h_scr[...] = h.astype(BF16)

    acc_scr[...] = _dot(h_scr[...], w_ref[...])

    def head_a(z, g, scale):
        y = z * _rms(z, HEAD_DIM) * g
        return _rope(y, ca_ref[...], sau_ref[...], sad_ref[...], 32) * scale

    def head_b(z, g, scale):
        lo = lax.broadcasted_iota(jnp.int32, (1, HEAD_DIM), 1) < B_DH
        z2 = z * z
        s_all = jnp.sum(z2, axis=-1, keepdims=True)
        s_lo = jnp.sum(jnp.where(lo, z2, 0.0), axis=-1, keepdims=True)
        ms = jnp.where(lo, s_lo, s_all - s_lo) * (1.0 / B_DH)
        y = z * lax.rsqrt(ms + EPS) * g
        return _rope(y, cb_ref[...], sbu_ref[...], sbd_ref[...], 16) * scale

    def heads(fn, out_ref, g_ref_, scale, n):
        for h in range(n):
            sl = slice(h * HEAD_DIM, (h + 1) * HEAD_DIM)
            out_ref[:, sl] = fn(acc_scr[:, sl], g_ref_[...], scale).astype(out_ref.dtype)

    @pl.when(j < 2)
    def _():
        heads(head_a, qa_ref, qna_ref, HEAD_DIM ** -0.5, 4)

    @pl.when(j == 2)
    def _():
        heads(head_a, kva_ref, kna_ref, 1.0, 2)
        kva_ref[:, 2 * HEAD_DIM:] = acc_scr[:, 2 * HEAD_DIM:].astype(BF16)

    @pl.when(j == 3)
    def _():
        heads(head_b, qb_ref, qnb_ref, B_DH ** -0.5, 4)

    @pl.when(j == 4)
    def _():
        heads(head_b, kb_ref, knb_ref, 1.0, 4)

    @pl.when(j == 5)
    def _():
        vb_ref[...] = acc_scr[...].astype(BF16)

    @pl.when(j == 6)
    def _():
        z = acc_scr[...]
        cq_ref[...] = (z * _sigmoid(z)).astype(BF16)

    def gates(lb_ref):
        z = acc_scr[...]
        lb = lb_ref[...]
        e = jnp.exp(-jnp.abs(z))
        inv = 1.0 / (1.0 + e)
        pos = z >= 0.0
        sig = jnp.where(pos, 1.0, e) * inv
        sig_neg = jnp.where(pos, e, 1.0) * inv
        ck_ref[...] = ((1.0 - lb) * sig_neg).astype(BF16)
        cl_ref[...] = jnp.log(lb + (1.0 - lb) * sig)

    @pl.when(j == 7)
    def _():
        gates(lbf_ref)

    @pl.when(j == 8)
    def _():
        gates(lbb_ref)

    @pl.when(j == 9)
    def _():
        cv_ref[...] = acc_scr[...].astype(BF16)

    @pl.when(j == 10)
    def _():
        z = acc_scr[...]
        cg_ref[...] = (z * _sigmoid(z)).astype(BF16)


def _in_call(x, modsel, norm_g, w_in, rope, qn_a, kn_a, qn_b, kn_b, lb_f, lb_b, n_ctx):
    b, s, d = x.shape
    tm = _pick(s, (640, 256, 128))
    ct = COL_TILE
    row = lambda bb, i, j: (bb, i, 0)
    tab = pl.BlockSpec((tm, HEAD_DIM), lambda bb, i, j: (i, 0))
    vec = lambda n: pl.BlockSpec((1, n), lambda bb, i, j: (0, 0))
    bf = lambda w: jax.ShapeDtypeStruct((b, s, w), BF16)
    out_shape = (bf(2 * ct), bf(ct), bf(ct), bf(ct), bf(ct), bf(ct),
                 jax.ShapeDtypeStruct((b, 2, s, ct), BF16), jax.ShapeDtypeStruct((b, 2, s, ct), F32),
                 bf(ct), bf(ct))
    blk = pl.BlockSpec((None, tm, ct), row)
    dir_blk = pl.BlockSpec((None, None, tm, ct), lambda bb, i, j: (bb, jnp.clip(j - 7, 0, 1), i, 0))
    out_specs = (pl.BlockSpec((None, tm, ct), lambda bb, i, j: (bb, i, jnp.minimum(j, 1))),
                 blk, blk, blk, blk, blk, dir_blk, dir_blk, blk, blk)
    return pl.pallas_call(
        functools.partial(_in_kernel, tm=tm, n_ctx=n_ctx),
        out_shape=out_shape,
        grid=(b, s // tm, N_COL_TILES),
        in_specs=[pl.BlockSpec((None, tm, d), row),
                  pl.BlockSpec((None, 2, N_MOD * d), lambda bb, i, j: (bb, 0, 0)),
                  vec(d),
                  pl.BlockSpec((d, ct), lambda bb, i, j: (0, j)),
                  tab, tab, tab, tab, tab, tab,
                  vec(HEAD_DIM), vec(HEAD_DIM), vec(HEAD_DIM), vec(HEAD_DIM), vec(ct), vec(ct)],
        out_specs=out_specs,
        scratch_shapes=[pltpu.VMEM((tm, d), BF16), pltpu.VMEM((tm, ct), F32)],
        compiler_params=_params(("arbitrary", "arbitrary", "arbitrary")),
        name="in_proj",
    )(x, modsel, norm_g, w_in, *rope, qn_a, kn_a, qn_b, kn_b, lb_f, lb_b)


def _softmax_step(s, m_ref, l_ref, acc_ref, v):
    m_prev = m_ref[...]
    m_new = jnp.maximum(m_prev, jnp.max(s, axis=-1, keepdims=True))
    p = jnp.exp(s - m_new)
    alpha = jnp.exp(m_prev - m_new)
    l_ref[...] = alpha * l_ref[...] + jnp.sum(p, axis=-1, keepdims=True)
    acc_ref[...] = alpha * acc_ref[...] + _dot(p.astype(BF16), v)
    m_ref[...] = m_new


def _kv_steps(i, tq, tk, n_ctx, s_len):
    return jnp.where(i * tq < n_ctx, n_ctx // tk, s_len // tk)


def _gqa_kernel(q_ref, k_ref, v_ref, o_ref, m_scr, l_scr, acc_scr, *, tq, tk, n_ctx, s_len):
    i = pl.program_id(2)
    m_scr[...] = jnp.full(m_scr.shape, -jnp.inf, F32)
    l_scr[...] = jnp.zeros(l_scr.shape, F32)
    acc_scr[...] = jnp.zeros(acc_scr.shape, F32)

    def body(j, carry):
        start = pl.multiple_of(j * tk, tk)
        kc = k_ref[pl.ds(start, tk), :]
        vc = v_ref[pl.ds(start, tk), :]
        for h in range(A_GROUP):
            q = q_ref[:, h * HEAD_DIM:(h + 1) * HEAD_DIM]
            _softmax_step(_dot_nt(q, kc), m_scr.at[h], l_scr.at[h], acc_scr.at[h], vc)
        return carry

    lax.fori_loop(0, _kv_steps(i, tq, tk, n_ctx, s_len), body, 0)
    for h in range(A_GROUP):
        o_ref[:, h * HEAD_DIM:(h + 1) * HEAD_DIM] = (acc_scr[h] / l_scr[h]).astype(o_ref.dtype)


def _gqa_call(qa, kva, n_ctx):
    b, s, _ = qa.shape
    tq = _pick(math.gcd(s, n_ctx), (256, 128))
    tk = tq
    gw = A_GROUP * HEAD_DIM
    return pl.pallas_call(
        functools.partial(_gqa_kernel, tq=tq, tk=tk, n_ctx=n_ctx, s_len=s),
        out_shape=jax.ShapeDtypeStruct((b, s, A_HEADS * HEAD_DIM), BF16),
        grid=(b, A_KV_HEADS, s // tq),
        in_specs=[pl.BlockSpec((None, tq, gw), lambda bb, g, i: (bb, i, g)),
                  pl.BlockSpec((None, s, HEAD_DIM), lambda bb, g, i: (bb, 0, g)),
                  pl.BlockSpec((None, s, HEAD_DIM), lambda bb, g, i: (bb, 0, A_KV_HEADS + g))],
        out_specs=pl.BlockSpec((None, tq, gw), lambda bb, g, i: (bb, i, g)),
        scratch_shapes=[pltpu.VMEM((A_GROUP, tq, 1), F32), pltpu.VMEM((A_GROUP, tq, 1), F32),
                        pltpu.VMEM((A_GROUP, tq, HEAD_DIM), F32)],
        compiler_params=_params(("arbitrary", "arbitrary", "arbitrary")),
        name="gqa_attn",
    )(qa, kva, kva)


def _diff_kernel(lam_ref, q_ref, k_ref, v_ref, g_ref, o_ref, m_scr, l_scr, acc_scr,
                 *, tq, tk, n_ctx, s_len, out_scale):
    i = pl.program_id(2)
    m_scr[...] = jnp.full(m_scr.shape, -jnp.inf, F32)
    l_scr[...] = jnp.zeros(l_scr.shape, F32)
    acc_scr[...] = jnp.zeros(acc_scr.shape, F32)
    lo = lax.broadcasted_iota(jnp.int32, (1, HEAD_DIM), 1) < B_DH
    q = q_ref[...]
    zero = jnp.zeros_like(q)
    q_maps = (jnp.where(lo, q, zero), jnp.where(lo, zero, q))

    def body(j, carry):
        start = pl.multiple_of(j * tk, tk)
        kc = k_ref[pl.ds(start, tk), :]
        vc = v_ref[pl.ds(start, tk), :]
        for m in range(2):
            _softmax_step(_dot_nt(q_maps[m], kc), m_scr.at[m], l_scr.at[m], acc_scr.at[m], vc)
        return carry

    lax.fori_loop(0, _kv_steps(i, tq, tk, n_ctx, s_len), body, 0)
    o = acc_scr[0] / l_scr[0] - lam_ref[0] * (acc_scr[1] / l_scr[1])
    o_ref[...] = (o * _rms(o, HEAD_DIM) * g_ref[...] * out_scale).astype(o_ref.dtype)


def _diff_call(lam, qb, kb, vb, subln_g, n_ctx, out_scale):
    b, s, _ = qb.shape
    tq = _pick(math.gcd(s, n_ctx), (256, 128))
    tk = tq
    return pl.pallas_call(
        functools.partial(_diff_kernel, tq=tq, tk=tk, n_ctx=n_ctx, s_len=s, out_scale=out_scale),
        out_shape=jax.ShapeDtypeStruct((b, s, B_HEADS * HEAD_DIM), BF16),
        grid=(b, B_HEADS, s // tq),
        in_specs=[pl.BlockSpec(memory_space=pltpu.SMEM),
                  pl.BlockSpec((None, tq, HEAD_DIM), lambda bb, h, i: (bb, i, h)),
                  pl.BlockSpec((None, s, HEAD_DIM), lambda bb, h, i: (bb, 0, h)),
                  pl.BlockSpec((None, s, HEAD_DIM), lambda bb, h, i: (bb, 0, h)),
                  pl.BlockSpec((1, HEAD_DIM), lambda bb, h, i: (0, 0))],
        out_specs=pl.BlockSpec((None, tq, HEAD_DIM), lambda bb, h, i: (bb, i, h)),
        scratch_shapes=[pltpu.VMEM((2, tq, 1), F32), pltpu.VMEM((2, tq, 1), F32),
                        pltpu.VMEM((2, tq, HEAD_DIM), F32)],
        compiler_params=_params(("arbitrary", "arbitrary", "arbitrary")),
        name="diff_attn",
    )(lam, qb, kb, vb, subln_g)


def _hgrn_constants(c):
    n_lev = int(math.log2(c))
    idx = np.arange(c)
    tri = np.zeros((2, c, c), np.float32)
    sel = np.zeros((2, n_lev + 1, c, c), np.float32)
    mask = np.zeros((2, n_lev + 1, c, c), np.float32)
    tri[0] = idx[:, None] >= idx[None, :]
    tri[1] = idx[:, None] <= idx[None, :]
    for lev in range(n_lev):
        bsz = 1 << lev
        pair = idx // (2 * bsz)
        right = (idx // bsz) % 2 == 1
        same_pair = pair[:, None] == pair[None, :]
        ref_f = pair * 2 * bsz + bsz - 1
        ref_b = pair * 2 * bsz + bsz
        sel[0, lev, idx, ref_f] = 1.0
        sel[1, lev, idx, ref_b] = 1.0
        mask[0, lev] = same_pair & right[:, None] & ~right[None, :]
        mask[1, lev] = same_pair & ~right[:, None] & right[None, :]
    sel[0, n_lev, :, c - 1] = 1.0
    sel[1, n_lev, :, 0] = 1.0
    mask[:, n_lev] = np.eye(c)
    return (jnp.asarray(tri, BF16), jnp.asarray(sel.reshape(2, (n_lev + 1) * c, c), BF16),
            jnp.asarray(mask, F32))


def _hgrn_kernel(q_ref, k_ref, lf_ref, v_ref, tri_ref, sel_ref, mask_ref, o_ref, st_scr, *, c):
    n_lev = mask_ref.shape[0] - 1

    @pl.when(pl.program_id(2) == 0)
    def _():
        st_scr[...] = jnp.zeros(st_scr.shape, F32)

    for h in range(C_HEADS):
        sl = slice(h * HEAD_DIM, (h + 1) * HEAD_DIM)
        q = q_ref[:, sl].astype(F32)
        k = k_ref[:, sl].astype(F32)
        v = v_ref[:, sl]
        cum = _dot_01(tri_ref[...], lf_ref[:, sl])
        refs = _dot_01(sel_ref[...], cum)
        a = _dot_nt(q.astype(BF16), k.astype(BF16)) * mask_ref[n_lev]
        for lev in range(n_lev):
            ref = refs[lev * c:(lev + 1) * c]
            qt = (q * jnp.exp(jnp.minimum(cum - ref, 0.0))).astype(BF16)
            kt = (k * jnp.exp(jnp.minimum(ref - cum, 0.0))).astype(BF16)
            a = a + _dot_nt(qt, kt) * mask_ref[lev]
        total = refs[n_lev * c:(n_lev + 1) * c]
        st = st_scr[h]
        q_in = (q * jnp.exp(cum)).astype(BF16)
        o_ref[:, sl] = _dot(a.astype(BF16), v) + _dot_nt(q_in, st.astype(BF16))
        k_out = (k * jnp.exp(total - cum)).astype(BF16)
        v_t = jnp.transpose(v.astype(F32)).astype(BF16)
        st_scr[h] = st * jnp.exp(total[0:1, :]) + _dot(v_t, k_out)


def _hgrn_call(cq, ck, cl, cv, consts, n_ctx):
    b, s, w = cq.shape
    c = HGRN_CHUNK
    n_chunks, n_ctx_chunks = s // c, n_ctx // c
    tri, sel, mask = consts

    def chunk(dd, n):
        back = jnp.where(n < n_ctx_chunks, n_ctx_chunks - 1 - n, n_chunks - 1 - (n - n_ctx_chunks))
        return jnp.where(dd == 0, n, back)

    shared = pl.BlockSpec((None, c, w), lambda bb, dd, n: (bb, chunk(dd, n), 0))
    per_dir = pl.BlockSpec((None, None, c, w), lambda bb, dd, n: (bb, dd, chunk(dd, n), 0))
    return pl.pallas_call(
        functools.partial(_hgrn_kernel, c=c),
        out_shape=jax.ShapeDtypeStruct((b, 2, s, w), F32),
        grid=(b, 2, n_chunks),
        in_specs=[shared, per_dir, per_dir, shared,
                  pl.BlockSpec((None, c, c), lambda bb, dd, n: (dd, 0, 0)),
                  pl.BlockSpec((None,) + sel.shape[1:], lambda bb, dd, n: (dd, 0, 0)),
                  pl.BlockSpec((None,) + mask.shape[1:], lambda bb, dd, n: (dd, 0, 0, 0))],
        out_specs=per_dir,
        scratch_shapes=[pltpu.VMEM((C_HEADS, HEAD_DIM, HEAD_DIM), F32)],
        compiler_params=_params(("arbitrary", "arbitrary", "arbitrary")),
        name="hgrn2_scan",
    )(cq, ck, cl, cv, tri, sel, mask)


def _out_kernel(x_ref, oa_ref, ob_ref, oh_ref, cg_ref, hg_ref, wo_ref, mod_ref, g2_ref, wq_ref,
                xn_ref, h2_ref, qp_ref, *, tm, n_ctx):
    i = pl.program_id(1)
    is_ctx, m_ctx, m_lat = _mod_rows(mod_ref, i * tm, tm, n_ctx)
    nq = A_HEADS * HEAD_DIM
    nb = B_HEADS * HEAD_DIM
    mix = _dot(oa_ref[...], wo_ref[0:nq, :]) + _dot(ob_ref[...], wo_ref[nq:nq + nb, :])
    for h in range(C_HEADS):
        sl = slice(h * HEAD_DIM, (h + 1) * HEAD_DIM)
        o = oh_ref[0, :, sl] + oh_ref[1, :, sl]
        oc = (o * _rms(o, HEAD_DIM) * hg_ref[...] * cg_ref[:, sl].astype(F32)).astype(BF16)
        mix = mix + _dot(oc, wo_ref[nq + nb + h * HEAD_DIM:nq + nb + (h + 1) * HEAD_DIM, :])
    x = x_ref[...] + _sel(is_ctx, m_ctx, m_lat, 2) * mix
    xn_ref[...] = x
    y = x * _rms(x, D_MODEL) * g2_ref[...]
    h2 = (y * (1.0 + _sel(is_ctx, m_ctx, m_lat, 4)) + _sel(is_ctx, m_ctx, m_lat, 3)).astype(BF16)
    h2_ref[...] = h2
    qp_ref[...] = _dot(h2, wq_ref[...])


def _out_call(x, oa, ob, oh, cg, hgrn_g, w_out, modsel, norm2_g, wq, n_ctx):
    b, s, d = x.shape
    tm = _pick(s, (256, 128))
    row = lambda bb, i: (bb, i, 0)
    const = lambda shape: pl.BlockSpec(shape, lambda bb, i: (0,) * len(shape))
    return pl.pallas_call(
        functools.partial(_out_kernel, tm=tm, n_ctx=n_ctx),
        out_shape=(jax.ShapeDtypeStruct((b, s, d), F32), jax.ShapeDtypeStruct((b, s, d), BF16),
                   jax.ShapeDtypeStruct((b, s, PEER_QW), F32)),
        grid=(b, s // tm),
        in_specs=[pl.BlockSpec((None, tm, d), row),
                  pl.BlockSpec((None, tm, oa.shape[-1]), row),
                  pl.BlockSpec((None, tm, ob.shape[-1]), row),
                  pl.BlockSpec((None, 2, tm, C_W), lambda bb, i: (bb, 0, i, 0)),
                  pl.BlockSpec((None, tm, C_W), row),
                  const((1, HEAD_DIM)), const((d, d)),
                  pl.BlockSpec((None, 2, N_MOD * d), lambda bb, i: (bb, 0, 0)),
                  const((1, d)), const((d, PEER_QW))],
        out_specs=(pl.BlockSpec((None, tm, d), row), pl.BlockSpec((None, tm, d), row),
                   pl.BlockSpec((None, tm, PEER_QW), row)),
        compiler_params=_params(("arbitrary", "arbitrary")),
        name="out_proj",
    )(x, oa, ob, oh, cg, hgrn_g, w_out, modsel, norm2_g, wq)


def _top_rows(s, n, out_scr):
    for r in range(n):
        m = jnp.max(s, axis=0, keepdims=True)
        out_scr[r:r + 1, :] = m
        s = jnp.where(s == m, -jnp.inf, s)


def _peer_kernel(h2_ref, qp_ref, k1_ref, k2_ref, u_ref, vt_ref, o_ref,
                 h2t_scr, s1_scr, s2_scr, tau_scr, inv_scr, top1_scr, top2_scr, acc_scr,
                 *, tb, ec):
    c = pl.program_id(1)
    n_i = ec // PEER_NKEYS

    @pl.when(c == 0)
    def _():
        h2t_scr[...] = jnp.transpose(h2_ref[...].astype(F32)).astype(BF16)
        qpt = jnp.transpose(qp_ref[...])
        for h in range(PEER_HEADS):
            base = h * 2 * PEER_HALF
            s1 = _dot(k1_ref[h], qpt[base:base + PEER_HALF].astype(BF16))
            s2 = _dot(k2_ref[h], qpt[base + PEER_HALF:base + 2 * PEER_HALF].astype(BF16))
            _top_rows(s1, PEER_TOPK, top1_scr)
            _top_rows(s2, PEER_TOPK, top2_scr)
            top = top1_scr[0:1, :] + top2_scr[0:1, :]
            a2 = top2_scr[...]
            cand = [(top1_scr[p:p + 1, :] - top) + a2 for p in range(PEER_TOPK)]
            zsum = jnp.zeros((1, tb), F32)
            tau = jnp.zeros((1, tb), F32)
            for _ in range(PEER_TOPK):
                m = cand[0]
                for t in cand[1:]:
                    m = jnp.maximum(m, t)
                tau = jnp.max(m, axis=0, keepdims=True)
                zsum = zsum + jnp.exp(tau)
                cand = [jnp.where(t == tau, -jnp.inf, t) for t in cand]
            s1_scr[h] = s1 - top
            s2_scr[h] = s2
            tau_scr[h:h + 1, :] = tau
            inv_scr[h:h + 1, :] = 1.0 / zsum
        acc_scr[...] = jnp.zeros(acc_scr.shape, F32)

    pre = _dot(u_ref[...], h2t_scr[...])
    act = 0.5 * pre * (1.0 + lax.erf(pre * (2.0 ** -0.5)))
    w_rows = []
    for ii in range(n_i):
        gate = jnp.zeros((PEER_NKEYS, tb), F32)
        for h in range(PEER_HEADS):
            val = s1_scr[h, pl.ds(c * n_i + ii, 1), :] + s2_scr[h]
            gate = gate + jnp.where(val >= tau_scr[h:h + 1, :], jnp.exp(val), 0.0) * inv_scr[h:h + 1, :]
        w_rows.append((gate * act[ii * PEER_NKEYS:(ii + 1) * PEER_NKEYS]).astype(BF16))
    acc_scr[...] += _dot(vt_ref[...], jnp.concatenate(w_rows, axis=0))

    @pl.when(c == pl.num_programs(1) - 1)
    def _():
        o_ref[...] = jnp.transpose(acc_scr[...])


def _peer_call(h2, qp, k1, k2, u, vt):
    n, d = h2.shape
    tb = _pick(n, (512, 256, 128))
    ec = 512
    return pl.pallas_call(
        functools.partial(_peer_kernel, tb=tb, ec=ec),
        out_shape=jax.ShapeDtypeStruct((n, d), F32),
        grid=(n // tb, PEER_EXPERTS // ec),
        in_specs=[pl.BlockSpec((tb, d), lambda t, c: (t, 0)),
                  pl.BlockSpec((tb, PEER_QW), lambda t, c: (t, 0)),
                  pl.BlockSpec((PEER_HEADS, PEER_NKEYS, PEER_HALF), lambda t, c: (0, 0, 0)),
                  pl.BlockSpec((PEER_HEADS, PEER_NKEYS, PEER_HALF), lambda t, c: (0, 0, 0)),
                  pl.BlockSpec((ec, d), lambda t, c: (c, 0)),
                  pl.BlockSpec((d, ec), lambda t, c: (0, c))],
        out_specs=pl.BlockSpec((tb, d), lambda t, c: (t, 0)),
        scratch_shapes=[pltpu.VMEM((d, tb), BF16),
                        pltpu.VMEM((PEER_HEADS, PEER_NKEYS, tb), F32),
                        pltpu.VMEM((PEER_HEADS, PEER_NKEYS, tb), F32),
                        pltpu.VMEM((PEER_HEADS, tb), F32), pltpu.VMEM((PEER_HEADS, tb), F32),
                        pltpu.VMEM((PEER_TOPK, tb), F32), pltpu.VMEM((PEER_TOPK, tb), F32),
                        pltpu.VMEM((d, tb), F32)],
        compiler_params=_params(("arbitrary", "arbitrary")),
        name="peer_ffn",
    )(h2, qp, k1, k2, u, vt)


def _res_kernel(x_ref, p_ref, mod_ref, o_ref, *, tm, n_ctx, row_off):
    is_ctx, m_ctx, m_lat = _mod_rows(mod_ref, row_off + pl.program_id(1) * tm, tm, n_ctx)
    o_ref[...] = x_ref[...] + _sel(is_ctx, m_ctx, m_lat, 5) * p_ref[...]


def _res_call(x, p, modsel, n_ctx, latent_only):
    b, s, d = x.shape
    tm = _pick(math.gcd(s, n_ctx), (256, 128))
    off = n_ctx // tm if latent_only else 0
    rows = s - n_ctx if latent_only else s
    src = pl.BlockSpec((None, tm, d), lambda bb, i: (bb, i + off, 0))
    return pl.pallas_call(
        functools.partial(_res_kernel, tm=tm, n_ctx=n_ctx, row_off=off * tm),
        out_shape=jax.ShapeDtypeStruct((b, rows, d), F32),
        grid=(b, rows // tm),
        in_specs=[src, src, pl.BlockSpec((None, 2, N_MOD * d), lambda bb, i: (bb, 0, 0))],
        out_specs=pl.BlockSpec((None, tm, d), lambda bb, i: (bb, i, 0)),
        compiler_params=_params(("arbitrary", "arbitrary")),
        name="gated_residual",
    )(x, p, modsel)


def _rope_tables(n_lat, n_ctx, dim):
    rows = n_lat // GRID_W
    row = jnp.repeat(jnp.arange(rows, dtype=F32), GRID_W)
    col = jnp.tile(jnp.arange(GRID_W, dtype=F32), rows)
    quarter = dim // 4
    freqs = ROPE_THETA ** (-jnp.arange(quarter, dtype=F32) / quarter)
    ang = jnp.stack([row[:, None] * freqs, col[:, None] * freqs], axis=1)
    ang = jnp.broadcast_to(ang[:, :, None, :], (n_lat, 2, 2, quarter)).reshape(n_lat, dim)
    ang = jnp.tile(ang, (1, HEAD_DIM // dim))
    ang = jnp.concatenate([jnp.zeros((n_ctx, HEAD_DIM), F32), ang], axis=0)
    first_half = (jnp.arange(HEAD_DIM) % (2 * quarter)) < quarter
    sin = jnp.sin(ang)
    return jnp.cos(ang), jnp.where(first_half, -sin, 0.0), jnp.where(first_half, 0.0, sin)


def kernel(x, c, ctx, c_ctx, w_ada, b_ada, norm1_g, w_in, qn_a, kn_a, qn_b, kn_b,
           lam_q1, lam_k1, lam_q2, lam_k2, subln_g, hgrn_lb_f, hgrn_lb_b, hgrn_g,
           w_out, norm2_g, peer_wq, peer_k1, peer_k2, peer_u, peer_v):
    b, n_lat, d = x.shape
    n_ctx = ctx.shape[1]
    depth = w_ada.shape[0]
    assert d == D_MODEL and n_ctx % HGRN_CHUNK == 0 and n_lat % HGRN_CHUNK == 0 and b + 1 <= 8

    cvec = jnp.zeros((8, d), F32).at[:b].set(c).at[b].set(c_ctx)
    mod = _ada_call(cvec, w_ada, b_ada)
    modsel = jnp.stack([jnp.broadcast_to(mod[:, b:b + 1], (depth, b, N_MOD * d)), mod[:, :b]], axis=2)

    rope = _rope_tables(n_lat, n_ctx, HEAD_DIM) + _rope_tables(n_lat, n_ctx, B_DH)
    hconsts = _hgrn_constants(HGRN_CHUNK)

    def lower_bounds(raw):
        sm = jax.nn.softmax(raw.astype(F32), axis=0)
        return jnp.cumsum(sm, axis=0) - sm[0:1]

    lb_f_all, lb_b_all = lower_bounds(hgrn_lb_f), lower_bounds(hgrn_lb_b)
    row = lambda a: a.reshape(1, -1).astype(F32)

    xs = jnp.concatenate([ctx, x], axis=1)
    for l in range(depth):
        lam_init = 0.8 - 0.6 * math.exp(-0.3 * l)
        lam = (jnp.exp(jnp.sum(lam_q1[l] * lam_k1[l])) - jnp.exp(jnp.sum(lam_q2[l] * lam_k2[l]))
               + lam_init).reshape(1).astype(F32)
        qa, kva, qb, kb, vb, cq, ck, cl, cv, cg = _in_call(
            xs, modsel[l], row(norm1_g[l]), w_in[l].astype(BF16), rope,
            row(qn_a[l]), row(kn_a[l]), row(jnp.tile(qn_b[l], 2)), row(jnp.tile(kn_b[l], 2)),
            row(lb_f_all[l]), row(lb_b_all[l]), n_ctx)
        oa = _gqa_call(qa, kva, n_ctx)
        ob = _diff_call(lam, qb, kb, vb, row(subln_g[l]), n_ctx, 1.0 - lam_init)
        oh = _hgrn_call(cq, ck, cl, cv, hconsts, n_ctx)
        xn, h2, qp = _out_call(xs, oa, ob, oh, cg, row(hgrn_g[l]), w_out[l].astype(BF16), modsel[l],
                               row(norm2_g[l]), peer_wq[l].astype(BF16), n_ctx)
        p = _peer_call(h2.reshape(-1, d), qp.reshape(-1, PEER_QW), peer_k1[l].astype(BF16),
                       peer_k2[l].astype(BF16), peer_u[l].astype(BF16),
                       jnp.transpose(peer_v[l]).astype(BF16)).reshape(xs.shape)
        xs = _res_call(xn, p, modsel[l], n_ctx, latent_only=(l == depth - 1))
    return xs
```

```python
import functools
import math

import numpy as np
import jax
import jax.numpy as jnp
from jax import lax
from jax.experimental import pallas as pl
from jax.experimental.pallas import tpu as pltpu

F32 = jnp.float32
BF16 = jnp.bfloat16

D_MODEL = 2048
HEAD_DIM = 128
GRID_W = 64
A_HEADS = 8
A_KV_HEADS = 2
A_GROUP = A_HEADS // A_KV_HEADS
B_HEADS = 4
B_DH = 64
C_HEADS = 4
C_W = C_HEADS * HEAD_DIM
ROPE_THETA = 10000.0
EPS = 1e-6
N_MOD = 6
PEER_HEADS = 8
PEER_NKEYS = 128
PEER_TOPK = 16
PEER_NTOP = PEER_TOPK + 1
PEER_HALF = 64
PEER_EXPERTS = PEER_NKEYS * PEER_NKEYS
PEER_QW = PEER_HEADS * 2 * PEER_HALF

COL_TILE = 512
N_COL_TILES = 11
HGRN_CHUNK = 128
LOG2E = 1.4426950408889634
SOFTMAX_SPAN_LIMIT = 80.0
V7X_VMEM_LIMIT = 56 * 1024 * 1024


def _pick(n, candidates):
    for c in candidates:
        if n % c == 0:
            return c
    raise ValueError(f"no block size in {candidates} divides {n}")


def _params(sem):
    return pltpu.CompilerParams(dimension_semantics=sem, vmem_limit_bytes=V7X_VMEM_LIMIT)


def _dot(a, b):
    return jnp.dot(a, b, preferred_element_type=F32)


def _dot_nt(a, b):
    return lax.dot_general(a, b, (((1,), (1,)), ((), ())), preferred_element_type=F32)


def _split3(x):
    hi = x.astype(BF16)
    r1 = x - hi.astype(F32)
    mid = r1.astype(BF16)
    lo = (r1 - mid.astype(F32)).astype(BF16)
    return hi, mid, lo


def _dot_01(p, x):
    hi, mid, lo = _split3(x)
    return _dot(p, hi) + _dot(p, mid) + _dot(p, lo)


def _sigmoid(z):
    return 1.0 / (1.0 + jnp.exp(-z))


def _mod_rows(mod_ref, row0, n_rows, n_ctx):
    rows = row0 + lax.broadcasted_iota(jnp.int32, (n_rows, 1), 0)
    is_ctx = rows < n_ctx
    return is_ctx, mod_ref[0:1, :], mod_ref[1:2, :]


def _sel(is_ctx, m_ctx, m_lat, k):
    d = D_MODEL
    return jnp.where(is_ctx, m_ctx[:, k * d:(k + 1) * d], m_lat[:, k * d:(k + 1) * d])


def _rms(x, width):
    return lax.rsqrt(jnp.sum(x * x, axis=-1, keepdims=True) * (1.0 / width) + EPS)


def _ada_kernel(c_ref, w_ref, b_ref, o_ref):
    c = c_ref[...]
    act = c * _sigmoid(c)
    o_ref[...] = _dot_hi(act, w_ref[...]) + b_ref[...]


def _dot_hi(a, w):
    a0, a1, a2 = _split3(a)
    w0, w1, w2 = _split3(w)
    return (_dot(a0, w0) + (_dot(a0, w1) + _dot(a1, w0))
            + (_dot(a1, w1) + _dot(a0, w2) + _dot(a2, w0)))


def _ada_call(cvec, w_ada, b_ada):
    depth, d, n = w_ada.shape
    tn = _pick(n, (1024, 512, 128))
    return pl.pallas_call(
        _ada_kernel,
        out_shape=jax.ShapeDtypeStruct((depth, 8, n), F32),
        grid=(depth, n // tn),
        in_specs=[pl.BlockSpec((8, d), lambda l, j: (0, 0)),
                  pl.BlockSpec((None, d, tn), lambda l, j: (l, 0, j)),
                  pl.BlockSpec((None, 1, tn), lambda l, j: (l, 0, j))],
        out_specs=pl.BlockSpec((None, 8, tn), lambda l, j: (l, 0, j)),
        compiler_params=_params(("arbitrary", "arbitrary")),
        name="adaln_mod",
    )(cvec, w_ada, b_ada.reshape(depth, 1, n))


def _rope(y, cos, s_up, s_dn, half):
    return y * cos + pltpu.roll(y, 128 - half, 1) * s_up + pltpu.roll(y, half, 1) * s_dn


def _in_kernel(x_ref, mod_ref, g_ref, w_ref, ca_ref, sau_ref, sad_ref, cb_ref, sbu_ref, sbd_ref,
               qna_ref, kna_ref, qnb_ref, knb_ref, lbf_ref, lbb_ref,
               qa_ref, ka_ref, va_ref, qb_ref, kb_ref, vb_ref, cq_ref, ck_ref, cl_ref, cv_ref, cg_ref,
               h_scr, acc_scr, *, tm, n_ctx):
    i = pl.program_id(1)
    j = pl.program_id(2)

    @pl.when(j == 0)
    def _():
        x = x_ref[...]
        is_ctx, m_ctx, m_lat = _mod_rows(mod_ref, i * tm, tm, n_ctx)
        y = x * _rms(x, D_MODEL) * g_ref[...]
        h = y * (1.0 + _sel(is_ctx, m_ctx, m_lat, 1)) + _sel(is_ctx, m_ctx, m_lat, 0)
        h_scr[...] = h.astype(BF16)

    acc_scr[...] = _dot(h_scr[...], w_ref[...])

    def head_a(z, g, scale):
        y = z * _rms(z, HEAD_DIM) * g
        return _rope(y, ca_ref[...], sau_ref[...], sad_ref[...], 32) * scale

    def head_b(z, g, scale):
        lo = lax.broadcasted_iota(jnp.int32, (1, HEAD_DIM), 1) < B_DH
        z2 = z * z
        s_all = jnp.sum(z2, axis=-1, keepdims=True)
        s_lo = jnp.sum(jnp.where(lo, z2, 0.0), axis=-1, keepdims=True)
        ms = jnp.where(lo, s_lo, s_all - s_lo) * (1.0 / B_DH)
        y = z * lax.rsqrt(ms + EPS) * g
        return _rope(y, cb_ref[...], sbu_ref[...], sbd_ref[...], 16) * scale

    def heads(fn, write, g_ref_, scale, n):
        for h in range(n):
            sl = slice(h * HEAD_DIM, (h + 1) * HEAD_DIM)
            write(h, fn(acc_scr[:, sl], g_ref_[...], scale).astype(BF16))

    def head_major(out_ref):
        def write(h, val):
            out_ref[h] = val
        return write

    def lane_major(out_ref):
        def write(h, val):
            out_ref[:, h * HEAD_DIM:(h + 1) * HEAD_DIM] = val
        return write

    def values_with_ones(out_ref, col0, n):
        ones = jnp.ones((tm, HEAD_DIM), BF16)
        for h in range(n):
            src_sl = slice(col0 + h * HEAD_DIM, col0 + (h + 1) * HEAD_DIM)
            out_ref[:, 2 * h * HEAD_DIM:(2 * h + 1) * HEAD_DIM] = acc_scr[:, src_sl].astype(BF16)
            out_ref[:, (2 * h + 1) * HEAD_DIM:(2 * h + 2) * HEAD_DIM] = ones

    @pl.when(j < 2)
    def _():
        heads(head_a, head_major(qa_ref), qna_ref, HEAD_DIM ** -0.5 * LOG2E, A_GROUP)

    @pl.when(j == 2)
    def _():
        heads(head_a, lane_major(ka_ref), kna_ref, 1.0, A_KV_HEADS)
        values_with_ones(va_ref, A_KV_HEADS * HEAD_DIM, A_KV_HEADS)

    @pl.when(j == 3)
    def _():
        heads(head_b, head_major(qb_ref), qnb_ref, B_DH ** -0.5 * LOG2E, B_HEADS)

    @pl.when(j == 4)
    def _():
        heads(head_b, lane_major(kb_ref), knb_ref, 1.0, B_HEADS)

    @pl.when(j == 5)
    def _():
        values_with_ones(vb_ref, 0, B_HEADS)

    @pl.when(j == 6)
    def _():
        z = acc_scr[...]
        cq_ref[...] = (z * _sigmoid(z)).astype(BF16)

    def gates(lb_ref):
        z = acc_scr[...]
        lb = lb_ref[...]
        e = jnp.exp(-jnp.abs(z))
        inv = 1.0 / (1.0 + e)
        pos = z >= 0.0
        sig = jnp.where(pos, 1.0, e) * inv
        sig_neg = jnp.where(pos, e, 1.0) * inv
        ck_ref[...] = ((1.0 - lb) * sig_neg).astype(BF16)
        cl_ref[...] = jnp.log(lb + (1.0 - lb) * sig)

    @pl.when(j == 7)
    def _():
        gates(lbf_ref)

    @pl.when(j == 8)
    def _():
        gates(lbb_ref)

    @pl.when(j == 9)
    def _():
        cv_ref[...] = acc_scr[...].astype(BF16)

    @pl.when(j == 10)
    def _():
        z = acc_scr[...]
        cg_ref[...] = (z * _sigmoid(z)).astype(BF16)


def _in_call(x, modsel, norm_g, w_in, rope, qn_a, kn_a, qn_b, kn_b, lb_f, lb_b, n_ctx):
    b, s, d = x.shape
    tm = _pick(s, (640, 256, 128))
    ct = COL_TILE
    row = lambda bb, i, j: (bb, i, 0)
    tab = pl.BlockSpec((tm, HEAD_DIM), lambda bb, i, j: (i, 0))
    vec = lambda n: pl.BlockSpec((1, n), lambda bb, i, j: (0, 0))
    bf = lambda w: jax.ShapeDtypeStruct((b, s, w), BF16)
    heads = lambda n: jax.ShapeDtypeStruct((b, n, s, HEAD_DIM), BF16)
    out_shape = (heads(A_HEADS), bf(A_KV_HEADS * HEAD_DIM), bf(2 * A_KV_HEADS * HEAD_DIM),
                 heads(B_HEADS), bf(ct), bf(2 * ct), bf(ct),
                 jax.ShapeDtypeStruct((b, 2, s, ct), BF16), jax.ShapeDtypeStruct((b, 2, s, ct), F32),
                 bf(ct), bf(ct))
    blk = lambda w: pl.BlockSpec((None, tm, w), row)
    dir_blk = pl.BlockSpec((None, None, tm, ct), lambda bb, i, j: (bb, jnp.clip(j - 7, 0, 1), i, 0))
    out_specs = (pl.BlockSpec((None, A_GROUP, tm, HEAD_DIM), lambda bb, i, j: (bb, jnp.minimum(j, 1), i, 0)),
                 blk(A_KV_HEADS * HEAD_DIM), blk(2 * A_KV_HEADS * HEAD_DIM),
                 pl.BlockSpec((None, B_HEADS, tm, HEAD_DIM), lambda bb, i, j: (bb, 0, i, 0)),
                 blk(ct), blk(2 * ct), blk(ct), dir_blk, dir_blk, blk(ct), blk(ct))
    return pl.pallas_call(
        functools.partial(_in_kernel, tm=tm, n_ctx=n_ctx),
        out_shape=out_shape,
        grid=(b, s // tm, N_COL_TILES),
        in_specs=[pl.BlockSpec((None, tm, d), row),
                  pl.BlockSpec((None, 2, N_MOD * d), lambda bb, i, j: (bb, 0, 0)),
                  vec(d),
                  pl.BlockSpec((d, ct), lambda bb, i, j: (0, j)),
                  tab, tab, tab, tab, tab, tab,
                  vec(HEAD_DIM), vec(HEAD_DIM), vec(HEAD_DIM), vec(HEAD_DIM), vec(ct), vec(ct)],
        out_specs=out_specs,
        scratch_shapes=[pltpu.VMEM((tm, d), BF16), pltpu.VMEM((tm, ct), F32)],
        compiler_params=_params(("arbitrary", "arbitrary", "arbitrary")),
        name="in_proj",
    )(x, modsel, norm_g, w_in, *rope, qn_a, kn_a, qn_b, kn_b, lb_f, lb_b)


def _attend(q, k_ref, v_ref, acc_scr, m_scr, bound, is_ctx_block, *, tk, n_ctx, s_len, online):
    acc_scr[...] = jnp.zeros(acc_scr.shape, F32)
    if online:
        m_scr[...] = jnp.full(m_scr.shape, -jnp.inf, F32)

    def chunk(start, size):
        s = _dot_nt(q, k_ref[pl.ds(start, size), :])
        vc = v_ref[pl.ds(start, size), :]
        if online:
            m_prev = m_scr[...]
            m_new = jnp.maximum(m_prev, jnp.max(s, axis=-1, keepdims=True))
            p = jnp.exp2(s - m_new).astype(BF16)
            acc_scr[...] = jnp.exp2(m_prev - m_new) * acc_scr[...] + _dot(p, vc)
            m_scr[...] = m_new
        else:
            acc_scr[...] += _dot(jnp.exp2(s - bound).astype(BF16), vc)

    @pl.when(is_ctx_block)
    def _():
        chunk(0, n_ctx)

    @pl.when(jnp.logical_not(is_ctx_block))
    def _():
        def body(j, carry):
            chunk(pl.multiple_of(j * tk, tk), tk)
            return carry
        lax.fori_loop(0, s_len // tk, body, 0)


def _gqa_kernel(bound_ref, q_ref, k_ref, v_ref, o_ref, acc_scr, m_scr, *, tq, n_ctx, **kw):
    q = q_ref[...].reshape(A_GROUP * tq, HEAD_DIM)
    _attend(q, k_ref, v_ref, acc_scr, m_scr, bound_ref[0], pl.program_id(2) * tq < n_ctx,
            n_ctx=n_ctx, **kw)
    for h in range(A_GROUP):
        rows = slice(h * tq, (h + 1) * tq)
        o = acc_scr[rows, :HEAD_DIM] / acc_scr[rows, HEAD_DIM:]
        o_ref[:, h * HEAD_DIM:(h + 1) * HEAD_DIM] = o.astype(o_ref.dtype)


def _attn_blocks(s, n_ctx):
    tq = _pick(math.gcd(s, n_ctx), (256, 128))
    tk = _pick(s, (1280, 256, 128))
    return tq, tk


def _gqa_call(bound, qa, ka, va, n_ctx, online):
    b, _, s, _ = qa.shape
    tq, tk = _attn_blocks(s, n_ctx)
    m_rows = A_GROUP * tq
    return pl.pallas_call(
        functools.partial(_gqa_kernel, tq=tq, tk=tk, n_ctx=n_ctx, s_len=s, online=online),
        out_shape=jax.ShapeDtypeStruct((b, s, A_HEADS * HEAD_DIM), BF16),
        grid=(b, A_KV_HEADS, s // tq),
        in_specs=[pl.BlockSpec(memory_space=pltpu.SMEM),
                  pl.BlockSpec((None, A_GROUP, tq, HEAD_DIM), lambda bb, g, i: (bb, g, i, 0)),
                  pl.BlockSpec((None, s, HEAD_DIM), lambda bb, g, i: (bb, 0, g)),
                  pl.BlockSpec((None, s, 2 * HEAD_DIM), lambda bb, g, i: (bb, 0, g))],
        out_specs=pl.BlockSpec((None, tq, A_GROUP * HEAD_DIM), lambda bb, g, i: (bb, i, g)),
        scratch_shapes=[pltpu.VMEM((m_rows, 2 * HEAD_DIM), F32), pltpu.VMEM((m_rows, 1), F32)],
        compiler_params=_params(("arbitrary", "arbitrary", "arbitrary")),
        name="gqa_attn_online" if online else "gqa_attn",
    )(bound, qa, ka, va)


def _diff_kernel(sc_ref, q_ref, k_ref, v_ref, g_ref, o_ref, acc_scr, m_scr, *, tq, n_ctx, out_scale, **kw):
    lo = lax.broadcasted_iota(jnp.int32, (1, HEAD_DIM), 1) < B_DH
    q = q_ref[...]
    zero = jnp.zeros_like(q)
    q2 = jnp.concatenate([jnp.where(lo, q, zero), jnp.where(lo, zero, q)], axis=0)
    _attend(q2, k_ref, v_ref, acc_scr, m_scr, sc_ref[0], pl.program_id(2) * tq < n_ctx,
            n_ctx=n_ctx, **kw)
    o1 = acc_scr[0:tq, :HEAD_DIM] / acc_scr[0:tq, HEAD_DIM:]
    o2 = acc_scr[tq:2 * tq, :HEAD_DIM] / acc_scr[tq:2 * tq, HEAD_DIM:]
    o = o1 - sc_ref[1] * o2
    o_ref[...] = (o * _rms(o, HEAD_DIM) * g_ref[...] * out_scale).astype(o_ref.dtype)


def _diff_call(bound_lam, qb, kb, vb, subln_g, n_ctx, out_scale, online):
    b, _, s, _ = qb.shape
    tq, tk = _attn_blocks(s, n_ctx)
    return pl.pallas_call(
        functools.partial(_diff_kernel, tq=tq, tk=tk, n_ctx=n_ctx, s_len=s, out_scale=out_scale,
                          online=online),
        out_shape=jax.ShapeDtypeStruct((b, s, B_HEADS * HEAD_DIM), BF16),
        grid=(b, B_HEADS, s // tq),
        in_specs=[pl.BlockSpec(memory_space=pltpu.SMEM),
                  pl.BlockSpec((None, None, tq, HEAD_DIM), lambda bb, h, i: (bb, h, i, 0)),
                  pl.BlockSpec((None, s, HEAD_DIM), lambda bb, h, i: (bb, 0, h)),
                  pl.BlockSpec((None, s, 2 * HEAD_DIM), lambda bb, h, i: (bb, 0, h)),
                  pl.BlockSpec((1, HEAD_DIM), lambda bb, h, i: (0, 0))],
        out_specs=pl.BlockSpec((None, tq, HEAD_DIM), lambda bb, h, i: (bb, i, h)),
        scratch_shapes=[pltpu.VMEM((2 * tq, 2 * HEAD_DIM), F32), pltpu.VMEM((2 * tq, 1), F32)],
        compiler_params=_params(("arbitrary", "arbitrary", "arbitrary")),
        name="diff_attn_online" if online else "diff_attn",
    )(bound_lam, qb, kb, vb, subln_g)


def _score_bound(g_q, g_k, width, scale):
    return (1.01 * LOG2E * scale * width) * jnp.max(jnp.abs(g_q)) * jnp.max(jnp.abs(g_k))


def _bounded_softmax_ok(bound):
    return 2.0 * bound <= SOFTMAX_SPAN_LIMIT * LOG2E


def _hgrn_constants(c):
    n_lev = int(math.log2(c))
    idx = np.arange(c)
    tri = np.zeros((2, c, c), np.float32)
    sel = np.zeros((2, n_lev + 1, c, c), np.float32)
    mask = np.zeros((2, n_lev + 1, c, c), np.float32)
    tri[0] = idx[:, None] >= idx[None, :]
    tri[1] = idx[:, None] <= idx[None, :]
    for lev in range(n_lev):
        bsz = 1 << lev
        pair = idx // (2 * bsz)
        right = (idx // bsz) % 2 == 1
        same_pair = pair[:, None] == pair[None, :]
        ref_f = pair * 2 * bsz + bsz - 1
        ref_b = pair * 2 * bsz + bsz
        sel[0, lev, idx, ref_f] = 1.0
        sel[1, lev, idx, ref_b] = 1.0
        mask[0, lev] = same_pair & right[:, None] & ~right[None, :]
        mask[1, lev] = same_pair & ~right[:, None] & right[None, :]
    sel[0, n_lev, :, c - 1] = 1.0
    sel[1, n_lev, :, 0] = 1.0
    mask[:, n_lev] = np.eye(c)
    return (jnp.asarray(tri, BF16), jnp.asarray(sel.reshape(2, (n_lev + 1) * c, c), BF16),
            jnp.asarray(mask, F32))


def _hgrn_kernel(q_ref, k_ref, lf_ref, v_ref, tri_ref, sel_ref, mask_ref, o_ref, st_scr, *, c):
    n_lev = mask_ref.shape[0] - 1

    @pl.when(pl.program_id(2) == 0)
    def _():
        st_scr[...] = jnp.zeros(st_scr.shape, F32)

    cum_all = _dot_01(tri_ref[...], lf_ref[...])
    refs_all = _dot_01(sel_ref[...], cum_all)
    for h in range(C_HEADS):
        sl = slice(h * HEAD_DIM, (h + 1) * HEAD_DIM)
        q = q_ref[:, sl].astype(F32)
        k = k_ref[:, sl].astype(F32)
        v = v_ref[:, sl]
        cum = cum_all[:, sl]
        refs = refs_all[:, sl]
        a = _dot_nt(q.astype(BF16), k.astype(BF16)) * mask_ref[n_lev]
        for lev in range(n_lev):
            ref = refs[lev * c:(lev + 1) * c]
            qt = (q * jnp.exp(jnp.minimum(cum - ref, 0.0))).astype(BF16)
            kt = (k * jnp.exp(jnp.minimum(ref - cum, 0.0))).astype(BF16)
            a = a + _dot_nt(qt, kt) * mask_ref[lev]
        total = refs[n_lev * c:(n_lev + 1) * c]
        st = st_scr[h]
        q_in = (q * jnp.exp(cum)).astype(BF16)
        o_ref[:, sl] = _dot(a.astype(BF16), v) + _dot_nt(q_in, st.astype(BF16))
        k_out = (k * jnp.exp(total - cum)).astype(BF16)
        v_t = jnp.transpose(v.astype(F32)).astype(BF16)
        st_scr[h] = st * jnp.exp(total[0:1, :]) + _dot(v_t, k_out)


def _hgrn_call(cq, ck, cl, cv, consts, n_ctx):
    b, s, w = cq.shape
    c = HGRN_CHUNK
    n_chunks, n_ctx_chunks = s // c, n_ctx // c
    tri, sel, mask = consts

    def chunk(dd, n):
        back = jnp.where(n < n_ctx_chunks, n_ctx_chunks - 1 - n, n_chunks - 1 - (n - n_ctx_chunks))
        return jnp.where(dd == 0, n, back)

    shared = pl.BlockSpec((None, c, w), lambda bb, dd, n: (bb, chunk(dd, n), 0))
    per_dir = pl.BlockSpec((None, None, c, w), lambda bb, dd, n: (bb, dd, chunk(dd, n), 0))
    return pl.pallas_call(
        functools.partial(_hgrn_kernel, c=c),
        out_shape=jax.ShapeDtypeStruct((b, 2, s, w), F32),
        grid=(b, 2, n_chunks),
        in_specs=[shared, per_dir, per_dir, shared,
                  pl.BlockSpec((None, c, c), lambda bb, dd, n: (dd, 0, 0)),
                  pl.BlockSpec((None,) + sel.shape[1:], lambda bb, dd, n: (dd, 0, 0)),
                  pl.BlockSpec((None,) + mask.shape[1:], lambda bb, dd, n: (dd, 0, 0, 0))],
        out_specs=per_dir,
        scratch_shapes=[pltpu.VMEM((C_HEADS, HEAD_DIM, HEAD_DIM), F32)],
        compiler_params=_params(("arbitrary", "arbitrary", "arbitrary")),
        name="hgrn2_scan",
    )(cq, ck, cl, cv, tri, sel, mask)


def _out_kernel(x_ref, oa_ref, ob_ref, oh_ref, cg_ref, hg_ref, wo_ref, mod_ref, g2_ref, wq_ref,
                xn_ref, h2_ref, qp_ref, *, tm, n_ctx):
    i = pl.program_id(1)
    is_ctx, m_ctx, m_lat = _mod_rows(mod_ref, i * tm, tm, n_ctx)
    nq = A_HEADS * HEAD_DIM
    nb = B_HEADS * HEAD_DIM
    mix = _dot(oa_ref[...], wo_ref[0:nq, :]) + _dot(ob_ref[...], wo_ref[nq:nq + nb, :])
    for h in range(C_HEADS):
        sl = slice(h * HEAD_DIM, (h + 1) * HEAD_DIM)
        o = oh_ref[0, :, sl] + oh_ref[1, :, sl]
        oc = (o * _rms(o, HEAD_DIM) * hg_ref[...] * cg_ref[:, sl].astype(F32)).astype(BF16)
        mix = mix + _dot(oc, wo_ref[nq + nb + h * HEAD_DIM:nq + nb + (h + 1) * HEAD_DIM, :])
    x = x_ref[...] + _sel(is_ctx, m_ctx, m_lat, 2) * mix
    xn_ref[...] = x
    y = x * _rms(x, D_MODEL) * g2_ref[...]
    h2 = (y * (1.0 + _sel(is_ctx, m_ctx, m_lat, 4)) + _sel(is_ctx, m_ctx, m_lat, 3)).astype(BF16)
    h2_ref[...] = h2
    qp_ref[...] = _dot(h2, wq_ref[...])


def _out_call(x, oa, ob, oh, cg, hgrn_g, w_out, modsel, norm2_g, wq, n_ctx):
    b, s, d = x.shape
    tm = _pick(s, (256, 128))
    row = lambda bb, i: (bb, i, 0)
    const = lambda shape: pl.BlockSpec(shape, lambda bb, i: (0,) * len(shape))
    return pl.pallas_call(
        functools.partial(_out_kernel, tm=tm, n_ctx=n_ctx),
        out_shape=(jax.ShapeDtypeStruct((b, s, d), F32), jax.ShapeDtypeStruct((b, s, d), BF16),
                   jax.ShapeDtypeStruct((b, s, PEER_QW), F32)),
        grid=(b, s // tm),
        in_specs=[pl.BlockSpec((None, tm, d), row),
                  pl.BlockSpec((None, tm, oa.shape[-1]), row),
                  pl.BlockSpec((None, tm, ob.shape[-1]), row),
                  pl.BlockSpec((None, 2, tm, C_W), lambda bb, i: (bb, 0, i, 0)),
                  pl.BlockSpec((None, tm, C_W), row),
                  const((1, HEAD_DIM)), const((d, d)),
                  pl.BlockSpec((None, 2, N_MOD * d), lambda bb, i: (bb, 0, 0)),
                  const((1, d)), const((d, PEER_QW))],
        out_specs=(pl.BlockSpec((None, tm, d), row), pl.BlockSpec((None, tm, d), row),
                   pl.BlockSpec((None, tm, PEER_QW), row)),
        compiler_params=_params(("arbitrary", "arbitrary")),
        name="out_proj",
    )(x, oa, ob, oh, cg, hgrn_g, w_out, modsel, norm2_g, wq)


def _top_rows(s, n, out_scr):
    for r in range(n):
        m = jnp.max(s, axis=0, keepdims=True)
        out_scr[r:r + 1, :] = m
        s = jnp.where(s == m, -jnp.inf, s)


def _pair_threshold(top1_scr, top2_scr, tb):
    top = top1_scr[0:1, :] + top2_scr[0:1, :]
    a1 = top1_scr[...] - top
    a2 = top2_scr[...]
    cand = [a1[0:1] + a2[0:8], a1[0:1] + a2[8:16], a1[0:1] + a2[16:24], a1[8:16] + a2[0:1],
            a1[16:24] + a2[0:1]] + [a1[p:p + 1] + a2[0:8] for p in range(1, 8)]
    zsum = jnp.zeros((1, tb), F32)
    taus = []
    for r in range(PEER_NTOP):
        m = cand[0]
        for t in cand[1:]:
            m = jnp.maximum(m, t)
        tau = jnp.max(m, axis=0, keepdims=True)
        taus.append(tau)
        if r < PEER_TOPK:
            zsum = zsum + jnp.exp(tau)
        cand = [jnp.where(t == tau, -jnp.inf, t) for t in cand]
    return top, 0.5 * (taus[PEER_TOPK - 1] + taus[PEER_TOPK]), zsum


def _peer_kernel(h2_ref, qp_ref, k1_ref, k2_ref, u_ref, vt_ref, o_ref,
                 h2t_scr, s1_scr, s2_scr, thr_scr, top1_scr, top2_scr, pre_scr, acc_scr,
                 *, tb, ec):
    c = pl.program_id(1)
    n_i = ec // PEER_NKEYS

    @pl.when(c == 0)
    def _():
        h2t_scr[...] = jnp.transpose(h2_ref[...].astype(F32)).astype(BF16)
        qpt = jnp.transpose(qp_ref[...])
        top1_scr[...] = jnp.full(top1_scr.shape, -jnp.inf, F32)
        top2_scr[...] = jnp.full(top2_scr.shape, -jnp.inf, F32)
        for h in range(PEER_HEADS):
            base = h * 2 * PEER_HALF
            s1 = _dot(k1_ref[h], qpt[base:base + PEER_HALF].astype(BF16))
            s2 = _dot(k2_ref[h], qpt[base + PEER_HALF:base + 2 * PEER_HALF].astype(BF16))
            _top_rows(s1, PEER_NTOP, top1_scr)
            _top_rows(s2, PEER_NTOP, top2_scr)
            top, theta, zsum = _pair_threshold(top1_scr, top2_scr, tb)
            log_z = jnp.log(zsum)
            s1_scr[h] = (s1 - (top + log_z)) * LOG2E
            s2_scr[h] = s2 * LOG2E
            thr_scr[h:h + 1, :] = (theta - log_z) * LOG2E
        pre_scr[...] = jnp.zeros(pre_scr.shape, F32)
        acc_scr[...] = jnp.zeros(acc_scr.shape, F32)

    row0 = jnp.maximum(c - 1, 0) * n_i

    def step(pre_w, pre_r):
        w_rows = []
        for ii in range(n_i):
            rows = slice(ii * PEER_NKEYS, (ii + 1) * PEER_NKEYS)
            pre_w[rows, :] = _dot(u_ref[rows, :], h2t_scr[...])
            gate = jnp.zeros((PEER_NKEYS, tb), F32)
            for h in range(PEER_HEADS):
                val = s1_scr[h, pl.ds(row0 + ii, 1), :] + s2_scr[h]
                gate = gate + jnp.where(val >= thr_scr[h:h + 1, :], jnp.exp2(val), 0.0)
            pre = pre_r[rows, :]
            act = 0.5 * pre * (1.0 + lax.erf(pre * (2.0 ** -0.5)))
            w_rows.append((gate * act).astype(BF16))
        half = ec // 2
        w_lo = jnp.concatenate(w_rows[:n_i // 2], axis=0)
        w_hi = jnp.concatenate(w_rows[n_i // 2:], axis=0)
        acc_scr[...] += _dot(vt_ref[:, :half], w_lo) + _dot(vt_ref[:, half:], w_hi)

    @pl.when(c % 2 == 0)
    def _():
        step(pre_scr.at[0], pre_scr.at[1])

    @pl.when(c % 2 == 1)
    def _():
        step(pre_scr.at[1], pre_scr.at[0])

    @pl.when(c == pl.num_programs(1) - 1)
    def _():
        o_ref[...] = jnp.transpose(acc_scr[...])


def _peer_call(h2, qp, k1, k2, u, vt):
    n, d = h2.shape
    tb = _pick(n, (512, 256, 128))
    ec = 512
    n_chunks = PEER_EXPERTS // ec
    top_rows = 24
    return pl.pallas_call(
        functools.partial(_peer_kernel, tb=tb, ec=ec),
        out_shape=jax.ShapeDtypeStruct((n, d), F32),
        grid=(n // tb, n_chunks + 1),
        in_specs=[pl.BlockSpec((tb, d), lambda t, c: (t, 0)),
                  pl.BlockSpec((tb, PEER_QW), lambda t, c: (t, 0)),
                  pl.BlockSpec((PEER_HEADS, PEER_NKEYS, PEER_HALF), lambda t, c: (0, 0, 0)),
                  pl.BlockSpec((PEER_HEADS, PEER_NKEYS, PEER_HALF), lambda t, c: (0, 0, 0)),
                  pl.BlockSpec((ec, d), lambda t, c: (jnp.minimum(c, n_chunks - 1), 0)),
                  pl.BlockSpec((d, ec), lambda t, c: (0, jnp.maximum(c - 1, 0)))],
        out_specs=pl.BlockSpec((tb, d), lambda t, c: (t, 0)),
        scratch_shapes=[pltpu.VMEM((d, tb), BF16),
                        pltpu.VMEM((PEER_HEADS, PEER_NKEYS, tb), F32),
                        pltpu.VMEM((PEER_HEADS, PEER_NKEYS, tb), F32),
                        pltpu.VMEM((PEER_HEADS, tb), F32),
                        pltpu.VMEM((top_rows, tb), F32), pltpu.VMEM((top_rows, tb), F32),
                        pltpu.VMEM((2, ec, tb), F32),
                        pltpu.VMEM((d, tb), F32)],
        compiler_params=_params(("arbitrary", "arbitrary")),
        name="peer_ffn",
    )(h2, qp, k1, k2, u, vt)


def _res_kernel(x_ref, p_ref, mod_ref, o_ref, *, tm, n_ctx, row_off):
    is_ctx, m_ctx, m_lat = _mod_rows(mod_ref, row_off + pl.program_id(1) * tm, tm, n_ctx)
    o_ref[...] = x_ref[...] + _sel(is_ctx, m_ctx, m_lat, 5) * p_ref[...]


def _res_call(x, p, modsel, n_ctx, latent_only):
    b, s, d = x.shape
    tm = _pick(math.gcd(s, n_ctx), (256, 128))
    off = n_ctx // tm if latent_only else 0
    rows = s - n_ctx if latent_only else s
    src = pl.BlockSpec((None, tm, d), lambda bb, i: (bb, i + off, 0))
    return pl.pallas_call(
        functools.partial(_res_kernel, tm=tm, n_ctx=n_ctx, row_off=off * tm),
        out_shape=jax.ShapeDtypeStruct((b, rows, d), F32),
        grid=(b, rows // tm),
        in_specs=[src, src, pl.BlockSpec((None, 2, N_MOD * d), lambda bb, i: (bb, 0, 0))],
        out_specs=pl.BlockSpec((None, tm, d), lambda bb, i: (bb, i, 0)),
        compiler_params=_params(("arbitrary", "arbitrary")),
        name="gated_residual",
    )(x, p, modsel)


def _rope_tables(n_lat, n_ctx, dim):
    rows = n_lat // GRID_W
    row = jnp.repeat(jnp.arange(rows, dtype=F32), GRID_W)
    col = jnp.tile(jnp.arange(GRID_W, dtype=F32), rows)
    quarter = dim // 4
    freqs = ROPE_THETA ** (-jnp.arange(quarter, dtype=F32) / quarter)
    ang = jnp.stack([row[:, None] * freqs, col[:, None] * freqs], axis=1)
    ang = jnp.broadcast_to(ang[:, :, None, :], (n_lat, 2, 2, quarter)).reshape(n_lat, dim)
    ang = jnp.tile(ang, (1, HEAD_DIM // dim))
    ang = jnp.concatenate([jnp.zeros((n_ctx, HEAD_DIM), F32), ang], axis=0)
    first_half = (jnp.arange(HEAD_DIM) % (2 * quarter)) < quarter
    sin = jnp.sin(ang)
    return jnp.cos(ang), jnp.where(first_half, -sin, 0.0), jnp.where(first_half, 0.0, sin)


def kernel(x, c, ctx, c_ctx, w_ada, b_ada, norm1_g, w_in, qn_a, kn_a, qn_b, kn_b,
           lam_q1, lam_k1, lam_q2, lam_k2, subln_g, hgrn_lb_f, hgrn_lb_b, hgrn_g,
           w_out, norm2_g, peer_wq, peer_k1, peer_k2, peer_u, peer_v):
    b, n_lat, d = x.shape
    n_ctx = ctx.shape[1]
    depth = w_ada.shape[0]
    assert d == D_MODEL and n_ctx % HGRN_CHUNK == 0 and n_lat % HGRN_CHUNK == 0 and b + 1 <= 8

    cvec = jnp.zeros((8, d), F32).at[:b].set(c).at[b].set(c_ctx)
    mod = _ada_call(cvec, w_ada, b_ada)
    modsel = jnp.stack([jnp.broadcast_to(mod[:, b:b + 1], (depth, b, N_MOD * d)), mod[:, :b]], axis=2)

    rope = _rope_tables(n_lat, n_ctx, HEAD_DIM) + _rope_tables(n_lat, n_ctx, B_DH)
    hconsts = _hgrn_constants(HGRN_CHUNK)

    def lower_bounds(raw):
        sm = jax.nn.softmax(raw.astype(F32), axis=0)
        return jnp.cumsum(sm, axis=0) - sm[0:1]

    lb_f_all, lb_b_all = lower_bounds(hgrn_lb_f), lower_bounds(hgrn_lb_b)
    row = lambda a: a.reshape(1, -1).astype(F32)

    xs = jnp.concatenate([ctx, x], axis=1)
    for l in range(depth):
        lam_init = 0.8 - 0.6 * math.exp(-0.3 * l)
        lam = (jnp.exp(jnp.sum(lam_q1[l] * lam_k1[l])) - jnp.exp(jnp.sum(lam_q2[l] * lam_k2[l]))
               + lam_init).astype(F32)
        qa, ka, va, qb, kb, vb, cq, ck, cl, cv, cg = _in_call(
            xs, modsel[l], row(norm1_g[l]), w_in[l].astype(BF16), rope,
            row(qn_a[l]), row(kn_a[l]), row(jnp.tile(qn_b[l], 2)), row(jnp.tile(kn_b[l], 2)),
            row(lb_f_all[l]), row(lb_b_all[l]), n_ctx)
        bound_a = _score_bound(qn_a[l], kn_a[l], HEAD_DIM, HEAD_DIM ** -0.5).astype(F32)
        oa = lax.cond(_bounded_softmax_ok(bound_a),
                      functools.partial(_gqa_call, n_ctx=n_ctx, online=False),
                      functools.partial(_gqa_call, n_ctx=n_ctx, online=True),
                      bound_a.reshape(1), qa, ka, va)
        bound_b = _score_bound(qn_b[l], kn_b[l], B_DH, B_DH ** -0.5).astype(F32)
        ob = lax.cond(_bounded_softmax_ok(bound_b),
                      functools.partial(_diff_call, n_ctx=n_ctx, out_scale=1.0 - lam_init, online=False),
                      functools.partial(_diff_call, n_ctx=n_ctx, out_scale=1.0 - lam_init, online=True),
                      jnp.stack([bound_b, lam]), qb, kb, vb, row(subln_g[l]))
        oh = _hgrn_call(cq, ck, cl, cv, hconsts, n_ctx)
        xn, h2, qp = _out_call(xs, oa, ob, oh, cg, row(hgrn_g[l]), w_out[l].astype(BF16), modsel[l],
                               row(norm2_g[l]), peer_wq[l].astype(BF16), n_ctx)
        p = _peer_call(h2.reshape(-1, d), qp.reshape(-1, PEER_QW), peer_k1[l].astype(BF16),
                       peer_k2[l].astype(BF16), peer_u[l].astype(BF16),
                       jnp.transpose(peer_v[l]).astype(BF16)).reshape(xs.shape)
        xs = _res_call(xn, p, modsel[l], n_ctx, latent_only=(l == depth - 1))
    return xs
```
